```python
import math
import jax
import jax.numpy as jnp
from jax import lax
import numpy as np

D_MODEL = 1024
BATCH = 8
SEQ = 2048
DEPTH = 4
DEC_BATCH = 32
DEC_SEQ = 8
PAST_LEN = 8192
PAGE_SIZE = 128

MIX_W = D_MODEL
CONV_W = MIX_W // 2
SB_W = MIX_W - CONV_W
SB_HEADS = 8
SB_HEAD_DIM = SB_W // SB_HEADS
SB_BIAS_INIT = -8.0
CONV_K = 3
SB_BLOCK = 128
N_MEM = 256
XA_HEADS = 4
XA_HEAD_DIM = 128
XA_W = XA_HEADS * XA_HEAD_DIM
FFN_DIM = ((8 * D_MODEL // 3 + 255) // 256) * 256
RMS_EPS = 1e-6

kernel_name = "hymba_conv_stickbreak_decoder_step"


def rmsnorm(x, g):
    xf = x.astype(jnp.float32)
    y = xf * lax.rsqrt(jnp.mean(xf * xf, axis=-1, keepdims=True) + RMS_EPS)
    return (y * g.astype(jnp.float32)).astype(x.dtype)


def dwconv3(u_full, w):
    T = u_full.shape[1] - (CONV_K - 1)
    return u_full[:, :T] * w[0] + u_full[:, 1:T + 1] * w[1] + u_full[:, 2:T + 2] * w[2]


def stick_breaking(q, k, v, bias, q_offset):
    B, Tq, H, d = q.shape
    Tk = k.shape[1]
    qb = math.gcd(Tq, SB_BLOCK)
    nb = Tq // qb
    qs = q.reshape(B, nb, qb, H, d).transpose(1, 0, 2, 3, 4)
    starts = jnp.arange(nb, dtype=jnp.int32) * qb
    key_pos = jnp.arange(Tk, dtype=jnp.int32)
    scale = 1.0 / math.sqrt(d)
    b_h = bias.astype(jnp.float32)[None, :, None, None]

    def block(args):
        qblk, start = args
        z = jnp.einsum('bqhd,bkhd->bhqk', qblk, k).astype(jnp.float32) * scale + b_h
        q_pos = q_offset + start + jnp.arange(qb, dtype=jnp.int32)
        mask = key_pos[None, :] < q_pos[:, None]
        log_1mb = jnp.where(mask, jax.nn.log_sigmoid(-z), 0.0)
        suffix = lax.cumsum(log_1mb, axis=3, reverse=True) - log_1mb
        a = jnp.where(mask, jnp.exp(jax.nn.log_sigmoid(z) + suffix), 0.0)
        return jnp.einsum('bhqk,bkhd->bqhd', a.astype(v.dtype), v)

    out = lax.map(block, (qs, starts))
    return out.transpose(1, 0, 2, 3, 4).reshape(B, Tq, H, d)


def cross_attn(h, mk, mv, w_q, w_o):
    B, T, _ = h.shape
    q = (h @ w_q).reshape(B, T, XA_HEADS, XA_HEAD_DIM)
    s = jnp.einsum('bqhd,bmhd->bhqm', q, mk).astype(jnp.float32) * (1.0 / math.sqrt(XA_HEAD_DIM))
    p = jax.nn.softmax(s, axis=-1)
    o = jnp.einsum('bhqm,bmhd->bqhd', p.astype(mv.dtype), mv).reshape(B, T, XA_W)
    return o @ w_o


def mem_kv(mem, g, w_k, w_v):
    B = mem.shape[0]
    m = rmsnorm(mem, g)
    mk = (m @ w_k).reshape(B, N_MEM, XA_HEADS, XA_HEAD_DIM)
    mv = (m @ w_v).reshape(B, N_MEM, XA_HEADS, XA_HEAD_DIM)
    return mk, mv


def trunk_layer(x, conv_prev, ffn_prev, k_past, v_past, mk, mv,
                w_in, conv_w, gn_conv, sb_bias, gn_sb, w_out, norm_mix, norm_xattn, w_xq, w_xo,
                norm_ffn, w_up, ffn_conv_w, w_down):
    B, T, _ = x.shape
    h = rmsnorm(x, norm_mix)
    proj = h @ w_in
    splits = [CONV_W, 2 * CONV_W, 3 * CONV_W, 3 * CONV_W + SB_W, 3 * CONV_W + 2 * SB_W]
    ch, cb, cc, q, k, v = jnp.split(proj, splits, axis=-1)
    u_full = jnp.concatenate([conv_prev, cc * ch], axis=1)
    y_conv = cb * dwconv3(u_full, conv_w)
    new_conv = u_full[:, -(CONV_K - 1):]
    q = q.reshape(B, T, SB_HEADS, SB_HEAD_DIM)
    k = k.reshape(B, T, SB_HEADS, SB_HEAD_DIM)
    v = v.reshape(B, T, SB_HEADS, SB_HEAD_DIM)
    k_all = jnp.concatenate([k_past, k], axis=1)
    v_all = jnp.concatenate([v_past, v], axis=1)
    y_sb = stick_breaking(q, k_all, v_all, sb_bias, k_past.shape[1]).reshape(B, T, SB_W)
    mix = jnp.concatenate([rmsnorm(y_conv, gn_conv), rmsnorm(y_sb, gn_sb)], axis=-1) @ w_out
    x = x + mix
    x = x + cross_attn(rmsnorm(x, norm_xattn), mk, mv, w_xq, w_xo)
    hf = rmsnorm(x, norm_ffn)
    a, b = jnp.split(hf @ w_up, [FFN_DIM], axis=-1)
    a_full = jnp.concatenate([ffn_prev, a], axis=1)
    x = x + (jax.nn.silu(dwconv3(a_full, ffn_conv_w)) * b) @ w_down
    new_ffn = a_full[:, -(CONV_K - 1):]
    return x, new_conv, new_ffn, k, v


def setup_inputs(seed: int = 0) -> dict:
    key = jax.random.key(seed)
    ks = jax.random.split(key, 32)
    f32 = jnp.float32
    n_pages = PAST_LEN // PAGE_SIZE
    n_phys = (DEC_BATCH * n_pages * 5) // 4

    def nrm(k, shape, scale=1.0):
        return jax.random.normal(k, shape, f32) * scale

    def gain(k, shape):
        return 1.0 + 0.02 * jax.random.normal(k, shape, f32)

    page_table = jax.random.permutation(ks[9], n_phys)[:DEC_BATCH * n_pages]
    page_table = page_table.reshape(DEC_BATCH, n_pages).astype(jnp.int32)
    return {
        "x_prompt": nrm(ks[0], (BATCH, SEQ, D_MODEL)),
        "x_sample": nrm(ks[1], (DEC_BATCH, DEC_SEQ, D_MODEL)),
        "cache_sb_k": nrm(ks[2], (DEPTH, n_phys, PAGE_SIZE, SB_HEADS, SB_HEAD_DIM)),
        "cache_sb_v": nrm(ks[3], (DEPTH, n_phys, PAGE_SIZE, SB_HEADS, SB_HEAD_DIM)),
        "cache_mem_k": nrm(ks[4], (DEPTH, DEC_BATCH, N_MEM, XA_HEADS, XA_HEAD_DIM)),
        "cache_mem_v": nrm(ks[5], (DEPTH, DEC_BATCH, N_MEM, XA_HEADS, XA_HEAD_DIM)),
        "state_conv": nrm(ks[6], (DEPTH, DEC_BATCH, CONV_K - 1, CONV_W)),
        "state_ffn_conv": nrm(ks[7], (DEPTH, DEC_BATCH, CONV_K - 1, FFN_DIM)),
        "page_table": page_table,
        "mem_prompt": nrm(ks[8], (BATCH, N_MEM, D_MODEL)),
        "w_in": nrm(ks[10], (DEPTH, D_MODEL, 3 * CONV_W + 3 * SB_W), D_MODEL ** -0.5),
        "conv_w": nrm(ks[11], (DEPTH, CONV_K, CONV_W), CONV_K ** -0.5),
        "gn_conv": gain(ks[12], (DEPTH, CONV_W)),
        "sb_bias": SB_BIAS_INIT + 0.5 * jax.random.normal(ks[27], (DEPTH, SB_HEADS), f32),
        "gn_sb": gain(ks[13], (DEPTH, SB_W)),
        "w_out": nrm(ks[14], (DEPTH, MIX_W, D_MODEL), MIX_W ** -0.5),
        "norm_mix": gain(ks[15], (DEPTH, D_MODEL)),
        "norm_xattn": gain(ks[16], (DEPTH, D_MODEL)),
        "norm_mem": gain(ks[17], (DEPTH, D_MODEL)),
        "w_xq": nrm(ks[18], (DEPTH, D_MODEL, XA_W), D_MODEL ** -0.5),
        "w_xk": nrm(ks[19], (DEPTH, D_MODEL, XA_W), D_MODEL ** -0.5),
        "w_xv": nrm(ks[20], (DEPTH, D_MODEL, XA_W), D_MODEL ** -0.5),
        "w_xo": nrm(ks[21], (DEPTH, XA_W, D_MODEL), XA_W ** -0.5),
        "norm_ffn": gain(ks[22], (DEPTH, D_MODEL)),
        "w_up": nrm(ks[23], (DEPTH, D_MODEL, 2 * FFN_DIM), D_MODEL ** -0.5),
        "ffn_conv_w": nrm(ks[24], (DEPTH, CONV_K, FFN_DIM), CONV_K ** -0.5),
        "w_down": nrm(ks[25], (DEPTH, FFN_DIM, D_MODEL), FFN_DIM ** -0.5),
        "norm_final": gain(ks[26], (D_MODEL,)),
    }


def reference(x_prompt, x_sample, cache_sb_k, cache_sb_v, cache_mem_k, cache_mem_v,
              state_conv, state_ffn_conv, page_table, mem_prompt,
              w_in, conv_w, gn_conv, sb_bias, gn_sb, w_out, norm_mix, norm_xattn, norm_mem,
              w_xq, w_xk, w_xv, w_xo, norm_ffn, w_up, ffn_conv_w, w_down, norm_final):
    bp = x_prompt.shape[0]
    bs = x_sample.shape[0]
    past = page_table.shape[1] * cache_sb_k.shape[2]
    dt = x_prompt.dtype
    conv0 = jnp.zeros((bp, CONV_K - 1, CONV_W), dt)
    ffn0 = jnp.zeros((bp, CONV_K - 1, FFN_DIM), dt)
    kv0 = jnp.zeros((bp, 0, SB_HEADS, SB_HEAD_DIM), dt)

    xp, xs = x_prompt, x_sample
    kp_l, vp_l, mkp_l, mvp_l, cp_l, fp_l = [], [], [], [], [], []
    ks_l, vs_l, cs_l, fs_l = [], [], [], []
    for l in range(DEPTH):
        params = (w_in[l], conv_w[l], gn_conv[l], sb_bias[l], gn_sb[l], w_out[l], norm_mix[l],
                  norm_xattn[l], w_xq[l], w_xo[l], norm_ffn[l], w_up[l], ffn_conv_w[l], w_down[l])
        mk, mv = mem_kv(mem_prompt, norm_mem[l], w_xk[l], w_xv[l])
        xp, cp, fp, kp, vp = trunk_layer(xp, conv0, ffn0, kv0, kv0, mk, mv, *params)
        kp_l.append(kp); vp_l.append(vp); mkp_l.append(mk); mvp_l.append(mv)
        cp_l.append(cp); fp_l.append(fp)
        k_past = cache_sb_k[l][page_table].reshape(bs, past, SB_HEADS, SB_HEAD_DIM)
        v_past = cache_sb_v[l][page_table].reshape(bs, past, SB_HEADS, SB_HEAD_DIM)
        xs, cs, fs, ksn, vsn = trunk_layer(xs, state_conv[l], state_ffn_conv[l], k_past, v_past,
                                           cache_mem_k[l], cache_mem_v[l], *params)
        ks_l.append(ksn); vs_l.append(vsn); cs_l.append(cs); fs_l.append(fs)

    y_prompt = rmsnorm(xp, norm_final)
    y_sample = rmsnorm(xs, norm_final)
    return (y_prompt, y_sample,
            jnp.stack(kp_l), jnp.stack(vp_l), jnp.stack(mkp_l), jnp.stack(mvp_l),
            jnp.stack(cp_l), jnp.stack(fp_l),
            jnp.stack(ks_l), jnp.stack(vs_l), jnp.stack(cs_l), jnp.stack(fs_l))
```

```python
import functools
import math

import jax
import jax.numpy as jnp
from jax import lax
from jax.experimental import pallas as pl
from jax.experimental.pallas import tpu as pltpu

RMS_EPS = 1e-6
CONV_K = 3
SB_HEADS = 8
SB_HEAD_DIM = 64
XA_HEADS = 4
XA_HEAD_DIM = 128

LANES = 128
SUBLANES = 8
VMEM_LIMIT_BYTES = 56 * 2**20

F32 = jnp.float32
BF16 = jnp.bfloat16


def _params(*semantics):
    return pltpu.CompilerParams(dimension_semantics=semantics, vmem_limit_bytes=VMEM_LIMIT_BYTES)


def _rms(x, g):
    return x * lax.rsqrt(jnp.mean(x * x, axis=-1, keepdims=True) + RMS_EPS) * g


def _dot(a, b):
    return jnp.dot(a, b, preferred_element_type=F32)


def _dot_nt(a, b):
    return lax.dot_general(a, b, (((1,), (1,)), ((), ())), preferred_element_type=F32)


def _log_one_minus_beta(z):
    return -(jnp.maximum(z, 0.0) + jnp.log1p(jnp.exp(-jnp.abs(z))))


def _suffix_sum_inclusive(l, tri):
    l_hi = l.astype(BF16)
    l_lo = (l - l_hi.astype(F32)).astype(BF16)
    return _dot(l_hi, tri) + _dot(l_lo, tri)


def _norm_matmul_kernel(x_ref, g_ref, w_ref, o_ref, hn_ref):
    @pl.when(pl.program_id(1) == 0)
    def _():
        hn_ref[...] = _rms(x_ref[...], g_ref[...]).astype(BF16)

    o_ref[...] = _dot(hn_ref[...], w_ref[...])


def norm_matmul(x, g, w, *, tm, tn):
    m, k = x.shape
    n = w.shape[1]
    assert m % tm == 0 and n % tn == 0
    return pl.pallas_call(
        _norm_matmul_kernel,
        grid=(m // tm, n // tn),
        in_specs=[
            pl.BlockSpec((tm, k), lambda i, j: (i, 0)),
            pl.BlockSpec((1, k), lambda i, j: (0, 0)),
            pl.BlockSpec((k, tn), lambda i, j: (0, j)),
        ],
        out_specs=pl.BlockSpec((tm, tn), lambda i, j: (i, j)),
        out_shape=jax.ShapeDtypeStruct((m, n), F32),
        scratch_shapes=[pltpu.VMEM((tm, k), BF16)],
        compiler_params=_params("arbitrary", "arbitrary"),
        name="norm_matmul",
    )(x, g.reshape(1, k), w)


def _sb_prompt_kernel(bias_ref, q_ref, k_ref, v_ref, o_ref, acc_ref, c_ref, *, blk, scale):
    hp = pl.program_id(1)
    i = pl.program_id(2)
    lane = lax.broadcasted_iota(jnp.int32, (1, LANES), 1)
    row = lax.broadcasted_iota(jnp.int32, (blk, blk), 0)
    col = lax.broadcasted_iota(jnp.int32, (blk, blk), 1)
    causal = col < row
    tri = (row >= col).astype(BF16)
    q2 = q_ref[0] * scale
    qs = [jnp.where((lane >= SB_HEAD_DIM * h) & (lane < SB_HEAD_DIM * (h + 1)), q2, 0.0).astype(BF16)
          for h in range(2)]
    bias = [bias_ref[2 * hp + h] for h in range(2)]

    def block(j, masked, first):
        kj = k_ref[0, pl.ds(j * blk, blk), :].astype(BF16)
        vj = v_ref[0, pl.ds(j * blk, blk), :].astype(BF16)
        for h in range(2):
            z = _dot_nt(qs[h], kj) + bias[h]
            l = _log_one_minus_beta(z)
            if masked:
                l = jnp.where(causal, l, 0.0)
            incl = _suffix_sum_inclusive(l, tri)
            e = z + incl
            if not first:
                e = e + c_ref[h]
            a = jnp.exp(e)
            if masked:
                a = jnp.where(causal, a, 0.0)
            pv = _dot(a.astype(BF16), vj)
            if first:
                acc_ref[h] = pv
                c_ref[h] = incl[:, 0:1]
            else:
                acc_ref[h] += pv
                c_ref[h] += incl[:, 0:1]

    block(i, True, True)

    def body(t, carry):
        block(i - 1 - t, False, False)
        return carry

    lax.fori_loop(0, i, body, 0)
    o_ref[0] = jnp.where(lane < SB_HEAD_DIM, acc_ref[0], acc_ref[1])


def sb_prompt(proj, bias, *, blk):
    b, t, _ = proj.shape
    assert t % blk == 0
    n_pairs = SB_HEADS // 2
    q_col = 1536 // LANES
    k_col = 2048 // LANES
    v_col = 2560 // LANES
    kern = functools.partial(_sb_prompt_kernel, blk=blk, scale=1.0 / math.sqrt(SB_HEAD_DIM))
    return pl.pallas_call(
        kern,
        grid=(b, n_pairs, t // blk),
        in_specs=[
            pl.BlockSpec(memory_space=pltpu.SMEM),
            pl.BlockSpec((1, blk, LANES), lambda bi, hp, i: (bi, i, q_col + hp)),
            pl.BlockSpec((1, t, LANES), lambda bi, hp, i: (bi, 0, k_col + hp)),
            pl.BlockSpec((1, t, LANES), lambda bi, hp, i: (bi, 0, v_col + hp)),
        ],
        out_specs=pl.BlockSpec((1, blk, LANES), lambda bi, hp, i: (bi, i, hp)),
        out_shape=jax.ShapeDtypeStruct((b, t, SB_HEADS * SB_HEAD_DIM), F32),
        scratch_shapes=[pltpu.VMEM((2, blk, LANES), F32), pltpu.VMEM((2, blk, 1), F32)],
        compiler_params=_params("arbitrary", "arbitrary", "arbitrary"),
        name="sb_prompt",
    )(bias, proj, proj, proj)


def _sb_decode_kernel(pt_ref, q_ref, kn_ref, vn_ref, bias_ref, *refs, n_pg, page, tq, scale):
    k_refs = refs[:n_pg]
    v_refs = refs[n_pg:2 * n_pg]
    o_ref = refs[2 * n_pg]
    qbd_ref, acc_ref, c_ref, kpad_ref, vpad_ref = refs[2 * n_pg + 1:]
    p = pl.program_id(1)
    rows = SB_HEADS * tq
    width = SB_HEADS * SB_HEAD_DIM
    row = lax.broadcasted_iota(jnp.int32, (page, page), 0)
    col = lax.broadcasted_iota(jnp.int32, (page, page), 1)
    tri = (row >= col).astype(BF16)
    bias = bias_ref[...]

    def block(kb, vb, mask, first, transposed):
        qk = _dot if transposed else _dot_nt
        av = _dot_nt if transposed else _dot
        z = qk(qbd_ref[...], kb) + bias
        l = _log_one_minus_beta(z)
        if mask is not None:
            l = jnp.where(mask, l, 0.0)
        incl = _suffix_sum_inclusive(l, tri)
        e = z + incl
        if not first:
            e = e + c_ref[...]
        a = jnp.exp(e)
        if mask is not None:
            a = jnp.where(mask, a, 0.0)
        pv = av(a.astype(BF16), vb)
        if first:
            acc_ref[...] = pv
            c_ref[...] = incl[:, 0:1]
        else:
            acc_ref[...] += pv
            c_ref[...] += incl[:, 0:1]

    @pl.when(p == 0)
    def _():
        q = q_ref[0] * scale
        r_head = lax.broadcasted_iota(jnp.int32, (rows, width), 0) // tq
        c_head = lax.broadcasted_iota(jnp.int32, (rows, width), 1) // SB_HEAD_DIM
        q_rep = jnp.concatenate([q] * SB_HEADS, axis=0)
        qbd_ref[...] = jnp.where(r_head == c_head, q_rep, 0.0).astype(BF16)
        kpad_ref[...] = jnp.zeros_like(kpad_ref)
        vpad_ref[...] = jnp.zeros_like(vpad_ref)
        kpad_ref[0:tq, :] = kn_ref[0]
        vpad_ref[0:tq, :] = vn_ref[0]
        qi = lax.broadcasted_iota(jnp.int32, (rows, page), 0) % tq
        kk = lax.broadcasted_iota(jnp.int32, (rows, page), 1)
        block(kpad_ref[...].astype(BF16), vpad_ref[...].astype(BF16), kk < qi, True, False)

    for r in range(n_pg):
        block(k_refs[r][0, 0].astype(BF16), v_refs[r][0, 0].astype(BF16), None, False, True)

    @pl.when(p == pl.num_programs(1) - 1)
    def _():
        lane_head = lax.broadcasted_iota(jnp.int32, (tq, width), 1) // SB_HEAD_DIM
        out = jnp.zeros((tq, width), F32)
        for h in range(SB_HEADS):
            out = jnp.where(lane_head == h, acc_ref[h * tq:(h + 1) * tq, :], out)
        o_ref[0] = out


def sb_decode(proj, bias, cache_k, cache_v, page_table, layer, *, n_pg):
    b, tq, _ = proj.shape
    _, _, width, page = cache_k.shape
    n_pages = page_table.shape[1]
    assert n_pages % n_pg == 0 and tq % SUBLANES == 0 and tq <= page
    rows = SB_HEADS * tq
    bias_rows = jnp.broadcast_to(jnp.repeat(bias, tq)[:, None], (rows, page))

    def page_map(r):
        def index(bi, p, pt):
            return (layer, pt[bi * n_pages + (n_pages - 1 - (p * n_pg + r))], 0, 0)
        return index

    cache_specs = [pl.BlockSpec((1, 1, width, page), page_map(r)) for r in range(n_pg)]
    kern = functools.partial(_sb_decode_kernel, n_pg=n_pg, page=page, tq=tq,
                             scale=1.0 / math.sqrt(SB_HEAD_DIM))
    grid_spec = pltpu.PrefetchScalarGridSpec(
        num_scalar_prefetch=1,
        grid=(b, n_pages // n_pg),
        in_specs=[
            pl.BlockSpec((1, tq, width), lambda bi, p, pt: (bi, 0, 3)),
            pl.BlockSpec((1, tq, width), lambda bi, p, pt: (bi, 0, 4)),
            pl.BlockSpec((1, tq, width), lambda bi, p, pt: (bi, 0, 5)),
            pl.BlockSpec((rows, page), lambda bi, p, pt: (0, 0)),
        ] + cache_specs + cache_specs,
        out_specs=pl.BlockSpec((1, tq, width), lambda bi, p, pt: (bi, 0, 0)),
        scratch_shapes=[
            pltpu.VMEM((rows, width), BF16),
            pltpu.VMEM((rows, width), F32),
            pltpu.VMEM((rows, 1), F32),
            pltpu.VMEM((page, width), F32),
            pltpu.VMEM((page, width), F32),
        ],
    )
    return pl.pallas_call(
        kern,
        grid_spec=grid_spec,
        out_shape=jax.ShapeDtypeStruct((b, tq, width), F32),
        compiler_params=_params("arbitrary", "arbitrary"),
        name="sb_decode",
    )(page_table.reshape(-1), proj, proj, proj, bias_rows, *([cache_k] * n_pg), *([cache_v] * n_pg))


def _causal_conv3(buf_ref, cur, w_ref, tm):
    return (buf_ref[:, SUBLANES - 2:SUBLANES - 2 + tm, :] * w_ref[0:1, :]
            + buf_ref[:, SUBLANES - 1:SUBLANES - 1 + tm, :] * w_ref[1:2, :]
            + cur * w_ref[2:3, :])


def _mix_out_kernel(ch_ref, cb_ref, cc_ref, ysb_ref, x_ref, prev_ref, cw_ref, gc_ref, gs_ref, w_ref,
                    o_ref, nc_ref, ubuf_ref, *, bb, tm):
    t = pl.program_id(1)
    cw = ch_ref.shape[-1]

    @pl.when(t == 0)
    def _():
        ubuf_ref[:, SUBLANES - 2:SUBLANES, :] = prev_ref[...]

    u = cc_ref[...] * ch_ref[...]
    ubuf_ref[:, SUBLANES:SUBLANES + tm, :] = u
    y_conv = cb_ref[...] * _causal_conv3(ubuf_ref, u, cw_ref, tm)
    last = ubuf_ref[:, SUBLANES + tm - 2:SUBLANES + tm, :]
    ubuf_ref[:, SUBLANES - 2:SUBLANES, :] = last
    nc_ref[...] = last

    yc = _rms(y_conv.reshape(bb * tm, cw), gc_ref[...]).astype(BF16)
    ys = _rms(ysb_ref[...].reshape(bb * tm, -1), gs_ref[...]).astype(BF16)
    mix = _dot(yc, w_ref[0:cw, :]) + _dot(ys, w_ref[cw:, :])
    o_ref[...] = x_ref[...] + mix.reshape(o_ref.shape)


def mix_out(proj, ysb, x, conv_prev, conv_w, gn_conv, gn_sb, w_out, *, bb, tm):
    b, t, d = x.shape
    cw = conv_w.shape[1]
    sw = ysb.shape[-1]
    assert b % bb == 0 and t % tm == 0 and tm % SUBLANES == 0
    kern = functools.partial(_mix_out_kernel, bb=bb, tm=tm)
    const2 = lambda bi, ti: (0, 0)
    return pl.pallas_call(
        kern,
        grid=(b // bb, t // tm),
        in_specs=[
            pl.BlockSpec((bb, tm, cw), lambda bi, ti: (bi, ti, 0)),
            pl.BlockSpec((bb, tm, cw), lambda bi, ti: (bi, ti, 1)),
            pl.BlockSpec((bb, tm, cw), lambda bi, ti: (bi, ti, 2)),
            pl.BlockSpec((bb, tm, sw), lambda bi, ti: (bi, ti, 0)),
            pl.BlockSpec((bb, tm, d), lambda bi, ti: (bi, ti, 0)),
            pl.BlockSpec((bb, CONV_K - 1, cw), lambda bi, ti: (bi, 0, 0)),
            pl.BlockSpec((CONV_K, cw), const2),
            pl.BlockSpec((1, cw), const2),
            pl.BlockSpec((1, sw), const2),
            pl.BlockSpec((cw + sw, d), const2),
        ],
        out_specs=[
            pl.BlockSpec((bb, tm, d), lambda bi, ti: (bi, ti, 0)),
            pl.BlockSpec((bb, CONV_K - 1, cw), lambda bi, ti: (bi, 0, 0)),
        ],
        out_shape=[
            jax.ShapeDtypeStruct((b, t, d), F32),
            jax.ShapeDtypeStruct((b, CONV_K - 1, cw), F32),
        ],
        scratch_shapes=[pltpu.VMEM((bb, SUBLANES + tm, cw), F32)],
        compiler_params=_params("arbitrary", "arbitrary"),
        name="mix_out",
    )(proj, proj, proj, ysb, x, conv_prev, conv_w, gn_conv.reshape(1, cw), gn_sb.reshape(1, sw), w_out)


def _xattn_kernel(x_ref, g_ref, wq_ref, mk_ref, mv_ref, wo_ref, o_ref, q_ref, a_ref, *, bb, tm, scale):
    d = x_ref.shape[-1]
    x = x_ref[...].reshape(bb * tm, d)
    xn = _rms(x, g_ref[...]).astype(BF16)
    q_ref[...] = _dot(xn, wq_ref[...]).reshape(q_ref.shape)

    def one_batch(bi, carry):
        q = q_ref[bi]
        mk = mk_ref[0, bi].astype(BF16)
        mv = mv_ref[0, bi].astype(BF16)
        heads = []
        for h in range(XA_HEADS):
            sl = slice(h * XA_HEAD_DIM, (h + 1) * XA_HEAD_DIM)
            s = _dot_nt(q[:, sl].astype(BF16), mk[:, sl]) * scale
            e = jnp.exp(s - jnp.max(s, axis=-1, keepdims=True))
            p = e / jnp.sum(e, axis=-1, keepdims=True)
            heads.append(_dot(p.astype(BF16), mv[:, sl]))
        a_ref[bi] = jnp.concatenate(heads, axis=-1)
        return carry

    lax.fori_loop(0, bb, one_batch, 0)
    att = a_ref[...].reshape(bb * tm, -1).astype(BF16)
    o_ref[...] = (x + _dot(att, wo_ref[...])).reshape(o_ref.shape)


def xattn(x, g, w_xq, mk, mv, layer, w_xo, *, bb, tm):
    b, t, d = x.shape
    n_mem, xw = mk.shape[2], mk.shape[3]
    assert b % bb == 0 and t % tm == 0 and tm % SUBLANES == 0
    kern = functools.partial(_xattn_kernel, bb=bb, tm=tm, scale=1.0 / math.sqrt(XA_HEAD_DIM))
    const2 = lambda bi, ti: (0, 0)
    return pl.pallas_call(
        kern,
        grid=(b // bb, t // tm),
        in_specs=[
            pl.BlockSpec((bb, tm, d), lambda bi, ti: (bi, ti, 0)),
            pl.BlockSpec((1, d), const2),
            pl.BlockSpec((d, xw), const2),
            pl.BlockSpec((1, bb, n_mem, xw), lambda bi, ti: (layer, bi, 0, 0)),
            pl.BlockSpec((1, bb, n_mem, xw), lambda bi, ti: (layer, bi, 0, 0)),
            pl.BlockSpec((xw, d), const2),
        ],
        out_specs=pl.BlockSpec((bb, tm, d), lambda bi, ti: (bi, ti, 0)),
        out_shape=jax.ShapeDtypeStruct((b, t, d), F32),
        scratch_shapes=[pltpu.VMEM((bb, tm, xw), F32), pltpu.VMEM((bb, tm, xw), F32)],
        compiler_params=_params("arbitrary", "arbitrary"),
        name="xattn",
    )(x, g.reshape(1, d), w_xq, mk, mv, w_xo)


def _ffn_kernel(x_ref, g_ref, wa_ref, wb_ref, cw_ref, wd_ref, prev_ref, o_ref, nf_ref,
                hn_ref, abuf_ref, carry_ref, *, bb, tm):
    t = pl.program_id(1)
    f = pl.program_id(2)
    d = x_ref.shape[-1]
    tf = wa_ref.shape[-1]

    @pl.when(f == 0)
    def _():
        x = x_ref[...]
        o_ref[...] = x
        hn_ref[...] = _rms(x.reshape(bb * tm, d), g_ref[...]).astype(BF16)

    @pl.when(t == 0)
    def _():
        carry_ref[f] = prev_ref[...]

    hn = hn_ref[...]
    a = _dot(hn, wa_ref[...]).reshape(bb, tm, tf)
    b = _dot(hn, wb_ref[...])
    abuf_ref[:, SUBLANES:SUBLANES + tm, :] = a
    abuf_ref[:, SUBLANES - 2:SUBLANES, :] = carry_ref[f]
    conv = _causal_conv3(abuf_ref, a, cw_ref, tm).reshape(bb * tm, tf)
    gate = conv / (1.0 + jnp.exp(-conv))
    last = abuf_ref[:, SUBLANES + tm - 2:SUBLANES + tm, :]
    carry_ref[f] = last
    nf_ref[...] = last
    o_ref[...] += _dot((gate * b).astype(BF16), wd_ref[...]).reshape(o_ref.shape)


def ffn(x, g, w_up, ffn_conv_w, w_down, ffn_prev, *, bb, tm, tf):
    b, t, d = x.shape
    fdim = w_down.shape[0]
    assert b % bb == 0 and t % tm == 0 and fdim % tf == 0 and tm % SUBLANES == 0
    nf = fdim // tf
    kern = functools.partial(_ffn_kernel, bb=bb, tm=tm)
    return pl.pallas_call(
        kern,
        grid=(b // bb, t // tm, nf),
        in_specs=[
            pl.BlockSpec((bb, tm, d), lambda bi, ti, fi: (bi, ti, 0)),
            pl.BlockSpec((1, d), lambda bi, ti, fi: (0, 0)),
            pl.BlockSpec((d, tf), lambda bi, ti, fi: (0, fi)),
            pl.BlockSpec((d, tf), lambda bi, ti, fi: (0, nf + fi)),
            pl.BlockSpec((CONV_K, tf), lambda bi, ti, fi: (0, fi)),
            pl.BlockSpec((tf, d), lambda bi, ti, fi: (fi, 0)),
            pl.BlockSpec((bb, CONV_K - 1, tf), lambda bi, ti, fi: (bi, 0, fi)),
        ],
        out_specs=[
            pl.BlockSpec((bb, tm, d), lambda bi, ti, fi: (bi, ti, 0)),
            pl.BlockSpec((bb, CONV_K - 1, tf), lambda bi, ti, fi: (bi, 0, fi)),
        ],
        out_shape=[
            jax.ShapeDtypeStruct((b, t, d), F32),
            jax.ShapeDtypeStruct((b, CONV_K - 1, fdim), F32),
        ],
        scratch_shapes=[
            pltpu.VMEM((bb * tm, d), BF16),
            pltpu.VMEM((bb, SUBLANES + tm, tf), F32),
            pltpu.VMEM((nf, bb, CONV_K - 1, tf), F32),
        ],
        compiler_params=_params("arbitrary", "arbitrary", "arbitrary"),
        name="ffn",
    )(x, g.reshape(1, d), w_up, w_up, ffn_conv_w, w_down, ffn_prev)


def _final_norm_kernel(x_ref, g_ref, o_ref):
    o_ref[...] = _rms(x_ref[...], g_ref[...])


def final_norm(x, g, *, tm):
    m, d = x.shape
    assert m % tm == 0
    return pl.pallas_call(
        _final_norm_kernel,
        grid=(m // tm,),
        in_specs=[pl.BlockSpec((tm, d), lambda i: (i, 0)), pl.BlockSpec((1, d), lambda i: (0, 0))],
        out_specs=pl.BlockSpec((tm, d), lambda i: (i, 0)),
        out_shape=jax.ShapeDtypeStruct((m, d), F32),
        compiler_params=_params("arbitrary"),
        name="final_norm",
    )(x, g.reshape(1, d))


def _tiles(b, t):
    if t >= 1024:
        return dict(rows=1024, bb=1, tm=1024, xa_bb=1, xa_tm=512)
    return dict(rows=b * t, bb=b, tm=t, xa_bb=8, xa_tm=t)


def _layer(x, conv_prev, ffn_prev, sb_fn, mk, mv, mem_layer, p, cfg):
    b, t, d = x.shape
    proj = norm_matmul(x.reshape(b * t, d), p["norm_mix"], p["w_in"], tm=cfg["rows"], tn=512)
    proj = proj.reshape(b, t, -1)
    ysb = sb_fn(proj)
    x, new_conv = mix_out(proj, ysb, x, conv_prev, p["conv_w"], p["gn_conv"], p["gn_sb"], p["w_out"],
                          bb=cfg["bb"], tm=cfg["tm"])
    x = xattn(x, p["norm_xattn"], p["w_xq"], mk, mv, mem_layer, p["w_xo"], bb=cfg["xa_bb"], tm=cfg["xa_tm"])
    x, new_ffn = ffn(x, p["norm_ffn"], p["w_up"], p["ffn_conv_w"], p["w_down"], ffn_prev,
                     bb=cfg["bb"], tm=cfg["tm"], tf=256)
    sbw = SB_HEADS * SB_HEAD_DIM
    k = proj[..., 2048:2048 + sbw].reshape(b, t, SB_HEADS, SB_HEAD_DIM)
    v = proj[..., 2048 + sbw:].reshape(b, t, SB_HEADS, SB_HEAD_DIM)
    return x, new_conv, new_ffn, k, v


def kernel(x_prompt, x_sample, cache_sb_k, cache_sb_v, cache_mem_k, cache_mem_v, state_conv, state_ffn_conv,
           page_table, mem_prompt, w_in, conv_w, gn_conv, sb_bias, gn_sb, w_out, norm_mix, norm_xattn, norm_mem,
           w_xq, w_xk, w_xv, w_xo, norm_ffn, w_up, ffn_conv_w, w_down, norm_final):
    depth = w_in.shape[0]
    bp, tp, d = x_prompt.shape
    bs, ts, _ = x_sample.shape
    n_mem = mem_prompt.shape[1]
    xw = XA_HEADS * XA_HEAD_DIM
    n_phys, page = cache_sb_k.shape[1], cache_sb_k.shape[2]
    cw = conv_w.shape[-1]
    fdim = w_down.shape[1]

    w_in, w_out, w_xq, w_xk, w_xv, w_xo, w_up, w_down = (
        w.astype(BF16) for w in (w_in, w_out, w_xq, w_xk, w_xv, w_xo, w_up, w_down))
    cache_k = jnp.transpose(cache_sb_k, (0, 1, 3, 4, 2)).reshape(depth, n_phys, -1, page)
    cache_v = jnp.transpose(cache_sb_v, (0, 1, 3, 4, 2)).reshape(depth, n_phys, -1, page)
    mem_k = cache_mem_k.reshape(depth, bs, n_mem, xw)
    mem_v = cache_mem_v.reshape(depth, bs, n_mem, xw)
    conv0 = jnp.zeros((bp, CONV_K - 1, cw), F32)
    ffn0 = jnp.zeros((bp, CONV_K - 1, fdim), F32)
    mem2d = mem_prompt.reshape(bp * n_mem, d)
    cfg_p = _tiles(bp, tp)
    cfg_s = _tiles(bs, ts)

    xp, xs = x_prompt, x_sample
    outs = [[] for _ in range(10)]
    for l in range(depth):
        p = dict(w_in=w_in[l], conv_w=conv_w[l], gn_conv=gn_conv[l], gn_sb=gn_sb[l], w_out=w_out[l],
                 norm_mix=norm_mix[l], norm_xattn=norm_xattn[l], w_xq=w_xq[l], w_xo=w_xo[l],
                 norm_ffn=norm_ffn[l], w_up=w_up[l], ffn_conv_w=ffn_conv_w[l], w_down=w_down[l])
        mk = norm_matmul(mem2d, norm_mem[l], w_xk[l], tm=bp * n_mem, tn=xw)
        mv = norm_matmul(mem2d, norm_mem[l], w_xv[l], tm=bp * n_mem, tn=xw)
        xp, cp, fp, kp, vp = _layer(
            xp, conv0, ffn0, functools.partial(sb_prompt, bias=sb_bias[l], blk=256),
            mk.reshape(1, bp, n_mem, xw), mv.reshape(1, bp, n_mem, xw), 0, p, cfg_p)
        xs, cs, fs, ksn, vsn = _layer(
            xs, state_conv[l], state_ffn_conv[l],
            functools.partial(sb_decode, bias=sb_bias[l], cache_k=cache_k, cache_v=cache_v,
                              page_table=page_table, layer=l, n_pg=8),
            mem_k, mem_v, l, p, cfg_s)
        for lst, val in zip(outs, (kp, vp, mk.reshape(bp, n_mem, XA_HEADS, XA_HEAD_DIM),
                                   mv.reshape(bp, n_mem, XA_HEADS, XA_HEAD_DIM), cp, fp, ksn, vsn, cs, fs)):
            lst.append(val)

    y_prompt = final_norm(xp.reshape(bp * tp, d), norm_final, tm=1024).reshape(bp, tp, d)
    y_sample = final_norm(xs.reshape(bs * ts, d), norm_final, tm=bs * ts).reshape(bs, ts, d)
    return (y_prompt, y_sample) + tuple(jnp.stack(o) for o in outs)
```
